```python
import math
import jax
import jax.numpy as jnp
from jax import lax
import numpy as np

D_MODEL = 2048
BATCH = 2
SEQ = 4096
DEPTH = 2

ROPE_THETA = 10000.0
ROPE_DIM = 64
Q_BLOCK = 128
NEG_INF = -1e30
LN_EPS = 1e-5
RMS_EPS = 1e-6

DIFF_HEADS = 8
DIFF_HEAD_DIM = 64
DIFF_V_DIM = 2 * DIFF_HEAD_DIM
MLA_HEADS = 8
MLA_Q_RANK = 512
MLA_KV_RANK = 256
MLA_NOPE_DIM = 128
MLA_ROPE_DIM = ROPE_DIM
MLA_V_DIM = 128
SWA_HEADS = 16
SWA_KV_HEADS = 4
SWA_GROUP = SWA_HEADS // SWA_KV_HEADS
SWA_HEAD_DIM = 64
WINDOW = 128

BRANCH_A_WIDTH = DIFF_HEADS * DIFF_V_DIM
BRANCH_B_WIDTH = MLA_HEADS * MLA_V_DIM
BRANCH_C_WIDTH = SWA_HEADS * SWA_HEAD_DIM

IN_SIZES = (
    2 * DIFF_HEADS * DIFF_HEAD_DIM,
    2 * DIFF_HEADS * DIFF_HEAD_DIM,
    DIFF_HEADS * DIFF_V_DIM,
    MLA_Q_RANK,
    MLA_KV_RANK,
    MLA_ROPE_DIM,
    SWA_HEADS * SWA_HEAD_DIM,
    SWA_KV_HEADS * SWA_HEAD_DIM,
    SWA_KV_HEADS * SWA_HEAD_DIM,
    3 * D_MODEL,
)
IN_WIDTH = sum(IN_SIZES)
IN_OFFSETS = tuple(int(o) for o in np.cumsum(IN_SIZES)[:-1])

N_EXPERTS = 16
N_GROUPS = 4
EXPERTS_PER_GROUP = N_EXPERTS // N_GROUPS
TOP_K = 2
EXPERT_HIDDEN = 512

DEEPNORM_ALPHA = (2 * DEPTH) ** 0.25
DEEPNORM_BETA = (8 * DEPTH) ** -0.25

kernel_name = 'hybrid_gated_diffattn_mla_swa_grouped_moe'


def layer_norm(x, g=None, b=None):
    xf = x.astype(jnp.float32)
    mu = xf.mean(-1, keepdims=True)
    var = jnp.square(xf - mu).mean(-1, keepdims=True)
    y = (xf - mu) * lax.rsqrt(var + LN_EPS)
    if g is not None:
        y = y * g.astype(jnp.float32) + b.astype(jnp.float32)
    return y.astype(x.dtype)


def rms_norm(x, g):
    xf = x.astype(jnp.float32)
    y = xf * lax.rsqrt(jnp.mean(xf * xf, -1, keepdims=True) + RMS_EPS) * g.astype(jnp.float32)
    return y.astype(x.dtype)


def modulate(xn, shift, scale):
    return xn * (1.0 + scale) + shift


def rope_tables(positions, dim):
    inv = ROPE_THETA ** (-jnp.arange(0, dim, 2, dtype=jnp.float32) / dim)
    ang = positions.astype(jnp.float32)[..., None] * inv
    return jnp.cos(ang), jnp.sin(ang)


def apply_rope(x, cos, sin):
    extra = x.ndim - 3
    cos = cos.reshape(cos.shape[:2] + (1,) * extra + cos.shape[2:])
    sin = sin.reshape(sin.shape[:2] + (1,) * extra + sin.shape[2:])
    x1, x2 = jnp.split(x.astype(jnp.float32), 2, axis=-1)
    out = jnp.concatenate([x1 * cos - x2 * sin, x2 * cos + x1 * sin], axis=-1)
    return out.astype(x.dtype)


def dense_attention(q, k, v, scale):
    B, M, H, S, dk = q.shape
    dv = v.shape[-1]
    nq = S // Q_BLOCK
    qb = q.reshape(B, M, H, nq, Q_BLOCK, dk).transpose(3, 0, 1, 2, 4, 5)

    def one_block(qi):
        s = jnp.einsum('bmhqd,bmhkd->bmhqk', qi, k).astype(jnp.float32) * scale
        p = jax.nn.softmax(s, axis=-1).astype(v.dtype)
        return jnp.einsum('bmhqk,bhkv->bmhqv', p, v)

    out = lax.map(one_block, qb)
    return out.transpose(1, 2, 3, 0, 4, 5).reshape(B, M, H, S, dv)


def diff_attention(q, k, v, cos, sin, lam_q, lam_k, subln_g, lam_init):
    B, S, _ = q.shape
    q = apply_rope(q.reshape(B, S, 2, DIFF_HEADS, DIFF_HEAD_DIM), cos, sin).transpose(0, 2, 3, 1, 4)
    k = apply_rope(k.reshape(B, S, 2, DIFF_HEADS, DIFF_HEAD_DIM), cos, sin).transpose(0, 2, 3, 1, 4)
    v = v.reshape(B, S, DIFF_HEADS, DIFF_V_DIM).transpose(0, 2, 1, 3)
    o = dense_attention(q, k, v, DIFF_HEAD_DIM ** -0.5)
    e = jnp.exp(jnp.sum(lam_q.astype(jnp.float32) * lam_k.astype(jnp.float32), axis=-1))
    lam = e[0] - e[1] + lam_init
    o = o[:, 0] - lam.astype(o.dtype) * o[:, 1]
    o = rms_norm(o, subln_g) * (1.0 - lam_init)
    return o.transpose(0, 2, 1, 3).reshape(B, S, BRANCH_A_WIDTH)


def latent_attention(cq, ckv, kr, cos, sin, q_norm, kv_norm, w_uq, w_ukv):
    B, S, _ = cq.shape
    q = (rms_norm(cq, q_norm) @ w_uq).reshape(B, S, MLA_HEADS, MLA_NOPE_DIM + MLA_ROPE_DIM)
    q = jnp.concatenate([q[..., :MLA_NOPE_DIM], apply_rope(q[..., MLA_NOPE_DIM:], cos, sin)], axis=-1)
    kv = (rms_norm(ckv, kv_norm) @ w_ukv).reshape(B, S, MLA_HEADS, MLA_NOPE_DIM + MLA_V_DIM)
    k_nope, v = kv[..., :MLA_NOPE_DIM], kv[..., MLA_NOPE_DIM:]
    k_rope = apply_rope(kr, cos, sin)
    k = jnp.concatenate(
        [k_nope, jnp.broadcast_to(k_rope[:, :, None, :], (B, S, MLA_HEADS, MLA_ROPE_DIM))], axis=-1)
    q = q.transpose(0, 2, 1, 3)[:, None]
    k = k.transpose(0, 2, 1, 3)[:, None]
    v = v.transpose(0, 2, 1, 3)
    o = dense_attention(q, k, v, (MLA_NOPE_DIM + MLA_ROPE_DIM) ** -0.5)[:, 0]
    return o.transpose(0, 2, 1, 3).reshape(B, S, BRANCH_B_WIDTH)


def window_attention(q, k, v, cos, sin, sink):
    B, S, _ = q.shape
    nb = S // WINDOW
    q = apply_rope(q.reshape(B, S, SWA_KV_HEADS, SWA_GROUP, SWA_HEAD_DIM), cos, sin)
    k = apply_rope(k.reshape(B, S, SWA_KV_HEADS, SWA_HEAD_DIM), cos, sin)
    v = v.reshape(B, S, SWA_KV_HEADS, SWA_HEAD_DIM)
    qb = q.reshape(B, nb, WINDOW, SWA_KV_HEADS, SWA_GROUP, SWA_HEAD_DIM)

    def band(t):
        tp = jnp.pad(t, ((0, 0), (WINDOW, WINDOW), (0, 0), (0, 0)))
        tp = tp.reshape(B, nb + 2, WINDOW, SWA_KV_HEADS, SWA_HEAD_DIM)
        return jnp.concatenate([tp[:, :-2], tp[:, 1:-1], tp[:, 2:]], axis=2)

    kb, vb = band(k), band(v)
    s = jnp.einsum('bnqgrd,bnkgd->bgrnqk', qb, kb).astype(jnp.float32) * (SWA_HEAD_DIM ** -0.5)
    blk = jnp.arange(nb)[:, None, None]
    qpos = blk * WINDOW + jnp.arange(WINDOW)[None, :, None]
    kpos = (blk - 1) * WINDOW + jnp.arange(3 * WINDOW)[None, None, :]
    valid = (jnp.abs(kpos - qpos) <= WINDOW) & (kpos >= 0) & (kpos < S)
    s = jnp.where(valid, s, NEG_INF)
    sink_logit = jnp.broadcast_to(
        sink.astype(jnp.float32).reshape(1, SWA_KV_HEADS, SWA_GROUP, 1, 1, 1), s.shape[:-1] + (1,))
    p = jax.nn.softmax(jnp.concatenate([s, sink_logit], axis=-1), axis=-1)[..., :-1].astype(v.dtype)
    o = jnp.einsum('bgrnqk,bnkgd->bnqgrd', p, vb)
    return o.reshape(B, S, BRANCH_C_WIDTH)


def hybrid_mixer(h, cos, sin, lam_init, w_in, lam_q, lam_k, subln_g, q_norm, kv_norm, w_uq, w_ukv,
                 sink, w_a, w_b, w_c, w_o):
    proj = h @ w_in
    a_q, a_k, a_v, b_cq, b_ckv, b_kr, c_q, c_k, c_v, gates = jnp.split(proj, IN_OFFSETS, axis=-1)
    ya = diff_attention(a_q, a_k, a_v, cos, sin, lam_q, lam_k, subln_g, lam_init)
    yb = latent_attention(b_cq, b_ckv, b_kr, cos, sin, q_norm, kv_norm, w_uq, w_ukv)
    yc = window_attention(c_q, c_k, c_v, cos, sin, sink)
    g_a, g_b, g_c = jnp.split(jax.nn.sigmoid(gates), 3, axis=-1)
    y = g_a * (ya @ w_a) + g_b * (yb @ w_b) + g_c * (yc @ w_c)
    return y @ w_o


def grouped_moe(h, router_w, router_bias, w1, w3, w2):
    B, S, D = h.shape
    t = h.reshape(B * S, D)
    T = t.shape[0]
    scores = jax.nn.sigmoid((t @ router_w).astype(jnp.float32))
    biased = scores + router_bias.astype(jnp.float32)
    grouped = biased.reshape(T, N_GROUPS, EXPERTS_PER_GROUP)
    group_score = lax.top_k(grouped, TOP_K)[0].sum(-1)
    best = jnp.argmax(group_score, axis=-1)
    in_group = jnp.arange(N_GROUPS)[None, :] == best[:, None]
    masked = jnp.where(in_group[:, :, None], grouped, NEG_INF).reshape(T, N_EXPERTS)
    _, idx = lax.top_k(masked, TOP_K)
    w = jnp.take_along_axis(scores, idx, axis=-1)
    w = w / jnp.sum(w, axis=-1, keepdims=True)
    gate = jnp.einsum('tk,tke->te', w, jax.nn.one_hot(idx, N_EXPERTS, dtype=jnp.float32))
    hidden = jax.nn.silu(jnp.einsum('td,edf->tef', t, w1)) * jnp.einsum('td,edf->tef', t, w3)
    hidden = hidden * gate[:, :, None].astype(hidden.dtype)
    out = jnp.einsum('tef,efd->td', hidden, w2)
    return out.reshape(B, S, D)


def setup_inputs(seed: int = 0) -> dict:
    key = jax.random.key(seed)
    ks = iter(jax.random.split(key, 40))
    L, D = DEPTH, D_MODEL

    def nrm(shape, std):
        return std * jax.random.normal(next(ks), shape, jnp.float32)

    return {
        'x': nrm((BATCH, SEQ, D), 1.0),
        'c': nrm((BATCH, D), 1.0),
        'positions': jnp.broadcast_to(jnp.arange(SEQ, dtype=jnp.int32), (BATCH, SEQ)),
        'w_in': nrm((L, D, IN_WIDTH), D ** -0.5),
        'diff_lambda_q': nrm((L, 2, DIFF_HEAD_DIM), 0.1),
        'diff_lambda_k': nrm((L, 2, DIFF_HEAD_DIM), 0.1),
        'diff_subln': 1.0 + nrm((L, DIFF_V_DIM), 0.02),
        'mla_q_norm': 1.0 + nrm((L, MLA_Q_RANK), 0.02),
        'mla_kv_norm': 1.0 + nrm((L, MLA_KV_RANK), 0.02),
        'mla_w_uq': nrm((L, MLA_Q_RANK, MLA_HEADS * (MLA_NOPE_DIM + MLA_ROPE_DIM)), MLA_Q_RANK ** -0.5),
        'mla_w_ukv': nrm((L, MLA_KV_RANK, MLA_HEADS * (MLA_NOPE_DIM + MLA_V_DIM)), MLA_KV_RANK ** -0.5),
        'swa_sink': nrm((L, SWA_HEADS), 1.0),
        'w_branch_a': nrm((L, BRANCH_A_WIDTH, D), DEEPNORM_BETA * BRANCH_A_WIDTH ** -0.5),
        'w_branch_b': nrm((L, BRANCH_B_WIDTH, D), DEEPNORM_BETA * BRANCH_B_WIDTH ** -0.5),
        'w_branch_c': nrm((L, BRANCH_C_WIDTH, D), DEEPNORM_BETA * BRANCH_C_WIDTH ** -0.5),
        'w_out': nrm((L, D, D), DEEPNORM_BETA * D ** -0.5),
        'w_ada': nrm((L, D, 6 * D), 0.5 * D ** -0.5),
        'b_ada': nrm((L, 6 * D), 0.1),
        'ln_mix_g': 1.0 + nrm((L, D), 0.02),
        'ln_mix_b': nrm((L, D), 0.02),
        'ln_ffn_g': 1.0 + nrm((L, D), 0.02),
        'ln_ffn_b': nrm((L, D), 0.02),
        'router_w': nrm((D, N_EXPERTS), D ** -0.5),
        'router_bias': nrm((N_EXPERTS,), 0.01),
        'expert_w1': nrm((L, N_EXPERTS, D, EXPERT_HIDDEN), D ** -0.5),
        'expert_w3': nrm((L, N_EXPERTS, D, EXPERT_HIDDEN), D ** -0.5),
        'expert_w2': nrm((L, N_EXPERTS, EXPERT_HIDDEN, D), DEEPNORM_BETA * EXPERT_HIDDEN ** -0.5),
    }


def reference(x, c, positions, w_in, diff_lambda_q, diff_lambda_k, diff_subln, mla_q_norm, mla_kv_norm,
              mla_w_uq, mla_w_ukv, swa_sink, w_branch_a, w_branch_b, w_branch_c, w_out, w_ada, b_ada,
              ln_mix_g, ln_mix_b, ln_ffn_g, ln_ffn_b, router_w, router_bias, expert_w1, expert_w3,
              expert_w2):
    cos, sin = rope_tables(positions, ROPE_DIM)
    c_act = jax.nn.silu(c)
    for l in range(DEPTH):
        mod = (c_act @ w_ada[l] + b_ada[l])[:, None, :]
        sh1, sc1, g1, sh2, sc2, g2 = jnp.split(mod, 6, axis=-1)
        lam_init = 0.8 - 0.6 * math.exp(-0.3 * l)
        h = modulate(layer_norm(x), sh1, sc1)
        y = hybrid_mixer(h, cos, sin, lam_init, w_in[l], diff_lambda_q[l], diff_lambda_k[l], diff_subln[l],
                         mla_q_norm[l], mla_kv_norm[l], mla_w_uq[l], mla_w_ukv[l], swa_sink[l],
                         w_branch_a[l], w_branch_b[l], w_branch_c[l], w_out[l])
        x = layer_norm(DEEPNORM_ALPHA * x + g1 * y, ln_mix_g[l], ln_mix_b[l])
        h = modulate(layer_norm(x), sh2, sc2)
        y = grouped_moe(h, router_w, router_bias, expert_w1[l], expert_w3[l], expert_w2[l])
        x = layer_norm(DEEPNORM_ALPHA * x + g2 * y, ln_ffn_g[l], ln_ffn_b[l])
    return x
```

```python
import functools
import math

import jax
import jax.numpy as jnp
from jax import lax
from jax.experimental import pallas as pl
from jax.experimental.pallas import tpu as pltpu

F32 = jnp.float32
BF16 = jnp.bfloat16

ROPE_THETA = 10000.0
ROPE_DIM = 64
NEG_INF = -1e30
LN_EPS = 1e-5
RMS_EPS = 1e-6
DIFF_HEADS = 8
DIFF_HEAD_DIM = 64
DIFF_V_DIM = 128
MLA_HEADS = 8
MLA_Q_RANK = 512
MLA_KV_RANK = 256
MLA_NOPE_DIM = 128
MLA_ROPE_DIM = 64
MLA_V_DIM = 128
SWA_HEADS = 16
SWA_KV_HEADS = 4
SWA_HEAD_DIM = 64
WINDOW = 128
N_EXPERTS = 16
N_GROUPS = 4
EXPERTS_PER_GROUP = 4
EXPERT_HIDDEN = 512

LANES = 128
HALF = 64
VMEM_LIMIT = 56 * 1024 * 1024


def _sigmoid(x):
    return 1.0 / (1.0 + jnp.exp(-x))


def _layer_norm(x):
    mu = jnp.mean(x, axis=-1, keepdims=True)
    xc = x - mu
    var = jnp.mean(xc * xc, axis=-1, keepdims=True)
    return xc * lax.rsqrt(var + LN_EPS)


def _rope_tile(x, cos, sin_signed, first):
    partner = jnp.where(first, pltpu.roll(x, 96, 1), pltpu.roll(x, 32, 1))
    return x * cos + partner * sin_signed


def _first_half_mask():
    lane = lax.broadcasted_iota(jnp.int32, (1, LANES), 1)
    return (lane % HALF) < (HALF // 2)


def _params(sem, vmem=VMEM_LIMIT):
    return pltpu.CompilerParams(dimension_semantics=sem, vmem_limit_bytes=vmem)


def _ada_kernel(c_ref, w_ref, b_ref, o_ref):
    c = c_ref[...]
    ca = (c * _sigmoid(c)).astype(BF16)
    o_ref[0] = jnp.dot(ca, w_ref[0].astype(BF16), preferred_element_type=F32) + b_ref[0]


def _ada_mod(c, w_ada, b_ada):
    L, D, N = w_ada.shape
    B = c.shape[0]
    rows = 8
    c8 = jnp.pad(c, ((0, rows - B), (0, 0)))
    tn = 1024
    out = pl.pallas_call(
        _ada_kernel,
        out_shape=jax.ShapeDtypeStruct((L, rows, N), F32),
        grid=(L, N // tn),
        in_specs=[
            pl.BlockSpec((rows, D), lambda l, j: (0, 0)),
            pl.BlockSpec((1, D, tn), lambda l, j: (l, 0, j)),
            pl.BlockSpec((1, 1, tn), lambda l, j: (l, 0, j)),
        ],
        out_specs=pl.BlockSpec((1, rows, tn), lambda l, j: (l, 0, j)),
        compiler_params=_params(("parallel", "parallel")),
        name="ada_mod",
    )(c8, w_ada, b_ada.reshape(L, 1, N))
    return out[:, :B].reshape(L, B, 6, D)


def _rope_table_kernel(pos_ref, cos_ref, sin_ref):
    pos = pos_ref[...].astype(F32)
    lane = lax.broadcasted_iota(jnp.int32, (1, LANES), 1)
    freq = (lane % (ROPE_DIM // 2)).astype(F32)
    inv = jnp.exp(freq * (-2.0 / ROPE_DIM * math.log(ROPE_THETA)))
    ang = pos * inv
    first = (lane % HALF) < (HALF // 2)
    s = jnp.sin(ang)
    cos_ref[...] = jnp.cos(ang)
    sin_ref[...] = jnp.where(first, -s, s)


def _rope_tables(positions):
    T = positions.size
    tm = min(T, 1024)
    return pl.pallas_call(
        _rope_table_kernel,
        out_shape=(jax.ShapeDtypeStruct((T, LANES), F32), jax.ShapeDtypeStruct((T, LANES), F32)),
        grid=(T // tm,),
        in_specs=[pl.BlockSpec((tm, 1), lambda i: (i, 0))],
        out_specs=(pl.BlockSpec((tm, LANES), lambda i: (i, 0)), pl.BlockSpec((tm, LANES), lambda i: (i, 0))),
        compiler_params=_params(("parallel",)),
        name="rope_tables",
    )(positions.reshape(T, 1))


def _inproj_kernel(x_ref, sh_ref, sc_ref, w_ref, cos_ref, sin_ref, o_ref, h_scr, *, mode):
    @pl.when(pl.program_id(1) == 0)
    def _():
        h = _layer_norm(x_ref[...]) * (1.0 + sc_ref[0]) + sh_ref[0]
        h_scr[...] = h.astype(BF16)

    acc = jnp.dot(h_scr[...], w_ref[...], preferred_element_type=F32)
    tn = acc.shape[1]
    if mode == "rope":
        first = _first_half_mask()
        cos = cos_ref[...]
        sin = sin_ref[...]
        for t in range(tn // LANES):
            sl = slice(t * LANES, (t + 1) * LANES)
            o_ref[:, sl] = _rope_tile(acc[:, sl], cos, sin, first).astype(o_ref.dtype)
    elif mode == "sigmoid":
        o_ref[...] = _sigmoid(acc).astype(o_ref.dtype)
    else:
        o_ref[...] = acc.astype(o_ref.dtype)


def _inproj(x2, sh, sc, w, cos, sin, *, mode, out_dtype, tm, tn, seq):
    T, D = x2.shape
    N = w.shape[1]
    per_batch = seq // tm
    return pl.pallas_call(
        functools.partial(_inproj_kernel, mode=mode),
        out_shape=jax.ShapeDtypeStruct((T, N), out_dtype),
        grid=(T // tm, N // tn),
        in_specs=[
            pl.BlockSpec((tm, D), lambda i, j: (i, 0)),
            pl.BlockSpec((1, 1, D), lambda i, j: (i // per_batch, 0, 0)),
            pl.BlockSpec((1, 1, D), lambda i, j: (i // per_batch, 0, 0)),
            pl.BlockSpec((D, tn), lambda i, j: (0, j)),
            pl.BlockSpec((tm, LANES), lambda i, j: (i, 0)),
            pl.BlockSpec((tm, LANES), lambda i, j: (i, 0)),
        ],
        out_specs=pl.BlockSpec((tm, tn), lambda i, j: (i, j)),
        scratch_shapes=[pltpu.VMEM((tm, D), BF16)],
        compiler_params=_params(("parallel", "arbitrary")),
        name="inproj_" + mode,
    )(x2, sh, sc, w, cos, sin)


def _rms(x, g):
    return x * lax.rsqrt(jnp.mean(x * x, axis=-1, keepdims=True) + RMS_EPS) * g


def _mla_up_kernel(lat_ref, qn_ref, kvn_ref, wq_ref, wkv_ref, cos_ref, sin_ref, q_ref, k_ref, v_ref, *, scale):
    lat = lat_ref[...]
    cq = lat[:, :MLA_Q_RANK]
    ckv = lat[:, MLA_Q_RANK:MLA_Q_RANK + MLA_KV_RANK]
    kr = lat[:, MLA_Q_RANK + MLA_KV_RANK:]
    q = jnp.dot(_rms(cq, qn_ref[...]).astype(BF16), wq_ref[...], preferred_element_type=F32)
    kv = jnp.dot(_rms(ckv, kvn_ref[...]).astype(BF16), wkv_ref[...], preferred_element_type=F32)
    cos = cos_ref[...]
    sin = sin_ref[...]
    first = _first_half_mask()
    lane = lax.broadcasted_iota(jnp.int32, (1, LANES), 1)
    low = lane < HALF
    k_rope = jnp.where(low, _rope_tile(kr, cos, sin, first), 0.0).astype(BF16)
    nope_w = MLA_HEADS * MLA_NOPE_DIM
    for t in range(MLA_HEADS // 2):
        r = _rope_tile(q[:, nope_w + t * LANES: nope_w + (t + 1) * LANES], cos, sin, first) * scale
        for a in range(2):
            h = 2 * t + a
            rh = r if a == 0 else pltpu.roll(r, HALF, 1)
            q_ref[h, :, :LANES] = (q[:, h * LANES:(h + 1) * LANES] * scale).astype(BF16)
            q_ref[h, :, LANES:] = jnp.where(low, rh, 0.0).astype(BF16)
    for h in range(MLA_HEADS):
        k_ref[h, :, :LANES] = kv[:, h * LANES:(h + 1) * LANES].astype(BF16)
        k_ref[h, :, LANES:] = k_rope
        v_ref[h] = kv[:, nope_w + h * LANES: nope_w + (h + 1) * LANES].astype(BF16)


def _mla_up(lat, qn, kvn, wq, wkv, cos, sin, *, tm):
    T, W = lat.shape
    H = MLA_HEADS
    scale = (MLA_NOPE_DIM + MLA_ROPE_DIM) ** -0.5
    full = lambda a: pl.BlockSpec(a.shape, lambda i: (0,) * a.ndim)
    return pl.pallas_call(
        functools.partial(_mla_up_kernel, scale=scale),
        out_shape=(jax.ShapeDtypeStruct((H, T, 2 * LANES), BF16),
                   jax.ShapeDtypeStruct((H, T, 2 * LANES), BF16),
                   jax.ShapeDtypeStruct((H, T, LANES), BF16)),
        grid=(T // tm,),
        in_specs=[pl.BlockSpec((tm, W), lambda i: (i, 0)), full(qn), full(kvn), full(wq), full(wkv),
                  pl.BlockSpec((tm, LANES), lambda i: (i, 0)), pl.BlockSpec((tm, LANES), lambda i: (i, 0))],
        out_specs=(pl.BlockSpec((H, tm, 2 * LANES), lambda i: (0, i, 0)),
                   pl.BlockSpec((H, tm, 2 * LANES), lambda i: (0, i, 0)),
                   pl.BlockSpec((H, tm, LANES), lambda i: (0, i, 0))),
        compiler_params=_params(("parallel",)),
        name="mla_up",
    )(lat, qn, kvn, wq, wkv, cos, sin)


def _flash(q, load_k, load_v, n_chunks, dv):
    tq = q.shape[0]

    def body(c, carry):
        m, l, acc = carry
        s = lax.dot_general(q, load_k(c), (((1,), (1,)), ((), ())), preferred_element_type=F32)
        m_new = jnp.maximum(m, jnp.max(s, axis=-1, keepdims=True))
        alpha = jnp.exp(m - m_new)
        p = jnp.exp(s - m_new)
        l = alpha * l + jnp.sum(p, axis=-1, keepdims=True)
        acc = alpha * acc + jnp.dot(p.astype(BF16), load_v(c), preferred_element_type=F32)
        return m_new, l, acc

    init = (jnp.full((tq, 1), NEG_INF, F32), jnp.zeros((tq, 1), F32), jnp.zeros((tq, dv), F32))
    _, l, acc = lax.fori_loop(0, n_chunks, body, init)
    return acc / l


def _attn_a_kernel(lam_ref, q1_ref, q2_ref, k1_ref, k2_ref, v_ref, lq_ref, lk_ref, g_ref, o_ref, *, tk):
    lam_init = lam_ref[0]
    e = jnp.exp(jnp.sum(lq_ref[...] * lk_ref[...], axis=-1, keepdims=True))
    lam = e[0:1, :] - e[1:2, :] + lam_init
    n_chunks = k1_ref.shape[0] // tk
    lane = lax.broadcasted_iota(jnp.int32, (1, LANES), 1)
    scale = DIFF_HEAD_DIM ** -0.5
    q1 = q1_ref[...] * scale
    q2 = q2_ref[...] * scale
    zero = jnp.zeros_like(q1)
    for a in range(2):
        msk = (lane < HALF) if a == 0 else (lane >= HALF)

        def load_v(c, a=a):
            return v_ref[pl.ds(pl.multiple_of(c * tk, tk), tk), a * LANES:(a + 1) * LANES]

        def load_k1(c):
            return k1_ref[pl.ds(pl.multiple_of(c * tk, tk), tk), :]

        def load_k2(c):
            return k2_ref[pl.ds(pl.multiple_of(c * tk, tk), tk), :]

        o1 = _flash(jnp.where(msk, q1, zero), load_k1, load_v, n_chunks, DIFF_V_DIM)
        o2 = _flash(jnp.where(msk, q2, zero), load_k2, load_v, n_chunks, DIFF_V_DIM)
        o = o1 - lam * o2
        y = _rms(o, g_ref[...]) * (1.0 - lam_init)
        o_ref[:, a * LANES:(a + 1) * LANES] = y.astype(o_ref.dtype)


def _attn_a(lam_arr, rope_out, plain_out, lq, lk, g, *, B, S, tq, tk):
    T = B * S
    nq = S // tq
    nk = S // S
    del nk
    return pl.pallas_call(
        functools.partial(_attn_a_kernel, tk=tk),
        out_shape=jax.ShapeDtypeStruct((T, DIFF_HEADS * DIFF_V_DIM), BF16),
        grid=(B, DIFF_HEADS // 2, nq),
        in_specs=[
            pl.BlockSpec(memory_space=pltpu.SMEM),
            pl.BlockSpec((tq, LANES), lambda b, j, i: (b * nq + i, j)),
            pl.BlockSpec((tq, LANES), lambda b, j, i: (b * nq + i, 4 + j)),
            pl.BlockSpec((S, LANES), lambda b, j, i: (b, 8 + j)),
            pl.BlockSpec((S, LANES), lambda b, j, i: (b, 12 + j)),
            pl.BlockSpec((S, 2 * LANES), lambda b, j, i: (b, j)),
            pl.BlockSpec(lq.shape, lambda b, j, i: (0, 0)),
            pl.BlockSpec(lk.shape, lambda b, j, i: (0, 0)),
            pl.BlockSpec(g.shape, lambda b, j, i: (0, 0)),
        ],
        out_specs=pl.BlockSpec((tq, 2 * LANES), lambda b, j, i: (b * nq + i, j)),
        compiler_params=_params(("parallel", "parallel", "arbitrary")),
        name="attn_diff",
    )(lam_arr, rope_out, rope_out, rope_out, rope_out, plain_out, lq, lk, g)


def _attn_b_kernel(q_ref, k_ref, v_ref, o_ref, *, tk):
    n_chunks = k_ref.shape[1] // tk

    def load_k(c):
        return k_ref[0, pl.ds(pl.multiple_of(c * tk, tk), tk), :]

    def load_v(c):
        return v_ref[0, pl.ds(pl.multiple_of(c * tk, tk), tk), :]

    o_ref[...] = _flash(q_ref[0], load_k, load_v, n_chunks, MLA_V_DIM).astype(o_ref.dtype)


def _attn_b(q, k, v, *, B, S, tq, tk):
    H, T, dk = q.shape
    nq = S // tq
    return pl.pallas_call(
        functools.partial(_attn_b_kernel, tk=tk),
        out_shape=jax.ShapeDtypeStruct((T, H * MLA_V_DIM), BF16),
        grid=(B, H, nq),
        in_specs=[
            pl.BlockSpec((1, tq, dk), lambda b, h, i: (h, b * nq + i, 0)),
            pl.BlockSpec((1, S, dk), lambda b, h, i: (h, b, 0)),
            pl.BlockSpec((1, S, MLA_V_DIM), lambda b, h, i: (h, b, 0)),
        ],
        out_specs=pl.BlockSpec((tq, MLA_V_DIM), lambda b, h, i: (b * nq + i, h)),
        compiler_params=_params(("parallel", "parallel", "arbitrary")),
        name="attn_mla",
    )(q, k, v)


def _attn_c_kernel(sink_ref, q_ref, kp_ref, ko_ref, kn_ref, ksp_ref, kso_ref, ksn_ref,
                   vp_ref, vo_ref, vn_ref, vsp_ref, vso_ref, vsn_ref, o_ref, *, nb):
    n = pl.program_id(1)
    jp = pl.program_id(2)
    W = WINDOW
    k_n = jnp.concatenate([kp_ref[...], ko_ref[...], kn_ref[...]], axis=0)
    k_s = jnp.concatenate([ksp_ref[...], kso_ref[...], ksn_ref[...]], axis=0)
    v_n = jnp.concatenate([vp_ref[...], vo_ref[...], vn_ref[...]], axis=0)
    v_s = jnp.concatenate([vsp_ref[...], vso_ref[...], vsn_ref[...]], axis=0)
    r = lax.broadcasted_iota(jnp.int32, (W, 3 * W), 0)
    c = lax.broadcasted_iota(jnp.int32, (W, 3 * W), 1)
    d = c - W - r
    kpos = (n - 1) * W + c
    valid = (d <= W) & (d >= -W) & (kpos >= 0) & (kpos < nb * W)
    lane = lax.broadcasted_iota(jnp.int32, (1, LANES), 1)
    low = lane < HALF
    scale = SWA_HEAD_DIM ** -0.5
    heads_per_block = 8
    for t in range(heads_per_block // 2):
        q_t = q_ref[:, t * LANES:(t + 1) * LANES] * scale
        zero = jnp.zeros_like(q_t)
        outs = []
        for a in range(2):
            qh = 2 * t + a
            kv_local = qh // 4
            same = (a == kv_local)
            kx = k_n if same else k_s
            vx = v_n if same else v_s
            msk = low if a == 0 else jnp.logical_not(low)
            s = lax.dot_general(jnp.where(msk, q_t, zero), kx, (((1,), (1,)), ((), ())),
                                preferred_element_type=F32)
            s = jnp.where(valid, s, NEG_INF)
            sink = sink_ref[jp * heads_per_block + qh]
            m = jnp.maximum(jnp.max(s, axis=-1, keepdims=True), sink)
            p = jnp.exp(s - m)
            l = jnp.sum(p, axis=-1, keepdims=True) + jnp.exp(sink - m)
            p = (p / l).astype(BF16)
            outs.append(jnp.dot(p, vx, preferred_element_type=F32))
        o_ref[:, t * LANES:(t + 1) * LANES] = jnp.where(low, outs[0], outs[1]).astype(o_ref.dtype)


def _attn_c(sink, rope_out, plain_out, *, B, S):
    T = B * S
    W = WINDOW
    nb = S // W
    qcol = 4
    kcol, kscol = 24, 26
    vcol, vscol = 8, 10

    def band(col):
        return [
            pl.BlockSpec((W, LANES), lambda b, n, j, col=col: (b * nb + jnp.maximum(n - 1, 0), col + j)),
            pl.BlockSpec((W, LANES), lambda b, n, j, col=col: (b * nb + n, col + j)),
            pl.BlockSpec((W, LANES), lambda b, n, j, col=col: (b * nb + jnp.minimum(n + 1, nb - 1), col + j)),
        ]

    return pl.pallas_call(
        functools.partial(_attn_c_kernel, nb=nb),
        out_shape=jax.ShapeDtypeStruct((T, SWA_HEADS * SWA_HEAD_DIM), BF16),
        grid=(B, nb, SWA_KV_HEADS // 2),
        in_specs=[pl.BlockSpec(memory_space=pltpu.SMEM),
                  pl.BlockSpec((W, 4 * LANES), lambda b, n, j: (b * nb + n, qcol + j))]
                 + band(kcol) + band(kscol) + band(vcol) + band(vscol),
        out_specs=pl.BlockSpec((W, 4 * LANES), lambda b, n, j: (b * nb + n, j)),
        compiler_params=_params(("parallel", "parallel", "parallel")),
        name="attn_swa",
    )(sink, rope_out, *([rope_out] * 6), *([plain_out] * 6))


def _merge_kernel(ya_ref, yb_ref, yc_ref, gt_ref, x_ref, g1_ref, wa_ref, wb_ref, wc_ref, wo_ref,
                  lg_ref, lb_ref, o_ref, *, alpha):
    D = x_ref.shape[1]
    y = gt_ref[:, 0:D].astype(F32) * jnp.dot(ya_ref[...], wa_ref[...], preferred_element_type=F32)
    y += gt_ref[:, D:2 * D].astype(F32) * jnp.dot(yb_ref[...], wb_ref[...], preferred_element_type=F32)
    y += gt_ref[:, 2 * D:3 * D].astype(F32) * jnp.dot(yc_ref[...], wc_ref[...], preferred_element_type=F32)
    out = jnp.dot(y.astype(BF16), wo_ref[...], preferred_element_type=F32)
    r = alpha * x_ref[...] + g1_ref[0] * out
    o_ref[...] = _layer_norm(r) * lg_ref[...] + lb_ref[...]


def _merge(ya, yb, yc, gates, x2, g1, wa, wb, wc, wo, lg, lb, *, alpha, tm, seq):
    T, D = x2.shape
    per_batch = seq // tm
    row = lambda w: pl.BlockSpec((tm, w), lambda i: (i, 0))
    resident = lambda a: pl.BlockSpec(a.shape, lambda i: (0, 0), pipeline_mode=pl.Buffered(1))
    return pl.pallas_call(
        functools.partial(_merge_kernel, alpha=alpha),
        out_shape=jax.ShapeDtypeStruct((T, D), F32),
        grid=(T // tm,),
        in_specs=[row(ya.shape[1]), row(yb.shape[1]), row(yc.shape[1]), row(gates.shape[1]), row(D),
                  pl.BlockSpec((1, 1, D), lambda i: (i // per_batch, 0, 0)),
                  resident(wa), resident(wb), resident(wc), resident(wo),
                  pl.BlockSpec((1, D), lambda i: (0, 0)), pl.BlockSpec((1, D), lambda i: (0, 0))],
        out_specs=row(D),
        compiler_params=_params(("parallel",)),
        name="merge_out",
    )(ya, yb, yc, gates, x2, g1, wa, wb, wc, wo, lg, lb)


def _route(scores, biased):
    sc = [scores[:, i:i + 1] for i in range(N_EXPERTS)]
    bi = [biased[:, i:i + 1] for i in range(N_EXPERTS)]
    gs = []
    for g in range(N_GROUPS):
        b = bi[4 * g:4 * g + 4]
        best_pair = None
        for i in range(4):
            for j in range(i + 1, 4):
                pair = b[i] + b[j]
                best_pair = pair if best_pair is None else jnp.maximum(best_pair, pair)
        gs.append(best_pair)
    best = jnp.zeros_like(gs[0]).astype(jnp.int32)
    best_v = gs[0]
    for g in range(1, N_GROUPS):
        upd = gs[g] > best_v
        best = jnp.where(upd, g, best)
        best_v = jnp.where(upd, gs[g], best_v)
    lane = lax.broadcasted_iota(jnp.int32, (1, LANES), 1)
    gate = jnp.zeros_like(scores)
    for g in range(N_GROUPS):
        b = bi[4 * g:4 * g + 4]
        s = sc[4 * g:4 * g + 4]
        sel = []
        for i in range(4):
            rank = jnp.zeros_like(b[i])
            for j in range(4):
                if j == i:
                    continue
                ahead = (b[j] > b[i]) if j > i else (b[j] >= b[i])
                rank = rank + jnp.where(ahead, 1.0, 0.0)
            sel.append(rank < 1.5)
        denom = sum(jnp.where(sel[i], s[i], 0.0) for i in range(4))
        in_group = best == g
        for i in range(4):
            w = jnp.where(sel[i] & in_group, s[i] / denom, 0.0)
            gate = gate + jnp.where(lane == 4 * g + i, w, 0.0)
    return gate


def _moe_kernel(x_ref, sh_ref, sc_ref, g2_ref, rwh_ref, rwl_ref, rb_ref, w1_ref, w3_ref, w2_ref,
                lg_ref, lb_ref, o_ref, h_scr, acc_scr, gate_scr, *, alpha):
    e = pl.program_id(1)

    @pl.when(e == 0)
    def _():
        h = _layer_norm(x_ref[...]) * (1.0 + sc_ref[0]) + sh_ref[0]
        hh = h.astype(BF16)
        hl = (h - hh.astype(F32)).astype(BF16)
        h_scr[...] = hh
        logits = (jnp.dot(hh, rwh_ref[...], preferred_element_type=F32)
                  + jnp.dot(hh, rwl_ref[...], preferred_element_type=F32)
                  + jnp.dot(hl, rwh_ref[...], preferred_element_type=F32))
        scores = _sigmoid(logits)
        gate_scr[...] = _route(scores, scores + rb_ref[...])
        acc_scr[...] = jnp.zeros_like(acc_scr)

    h = h_scr[...]
    a = jnp.dot(h, w1_ref[0], preferred_element_type=F32)
    b = jnp.dot(h, w3_ref[0], preferred_element_type=F32)
    lane = lax.broadcasted_iota(jnp.int32, (1, LANES), 1)
    g = jnp.sum(jnp.where(lane == e, gate_scr[...], 0.0), axis=-1, keepdims=True)
    hid = (a * _sigmoid(a)) * b * g
    acc_scr[...] += jnp.dot(hid.astype(BF16), w2_ref[0], preferred_element_type=F32)

    @pl.when(e == pl.num_programs(1) - 1)
    def _():
        r = alpha * x_ref[...] + g2_ref[0] * acc_scr[...]
        o_ref[...] = _layer_norm(r) * lg_ref[...] + lb_ref[...]


def _moe(x2, sh, sc, g2, rwh, rwl, rb, w1, w3, w2, lg, lb, *, alpha, tm, seq):
    T, D = x2.shape
    E, _, Fh = w1.shape
    per_batch = seq // tm
    modspec = pl.BlockSpec((1, 1, D), lambda i, e: (i // per_batch, 0, 0))
    const = lambda a: pl.BlockSpec(a.shape, lambda i, e: (0, 0))
    return pl.pallas_call(
        functools.partial(_moe_kernel, alpha=alpha),
        out_shape=jax.ShapeDtypeStruct((T, D), F32),
        grid=(T // tm, E),
        in_specs=[pl.BlockSpec((tm, D), lambda i, e: (i, 0)), modspec, modspec, modspec,
                  const(rwh), const(rwl), const(rb),
                  pl.BlockSpec((1, D, Fh), lambda i, e: (e, 0, 0)),
                  pl.BlockSpec((1, D, Fh), lambda i, e: (e, 0, 0)),
                  pl.BlockSpec((1, Fh, D), lambda i, e: (e, 0, 0)),
                  const(lg), const(lb)],
        out_specs=pl.BlockSpec((tm, D), lambda i, e: (i, 0)),
        scratch_shapes=[pltpu.VMEM((tm, D), BF16), pltpu.VMEM((tm, D), F32), pltpu.VMEM((tm, LANES), F32)],
        compiler_params=_params(("parallel", "arbitrary")),
        name="moe",
    )(x2, sh, sc, g2, rwh, rwl, rb, w1, w3, w2, lg, lb)


def _swap_halves(w):
    d, n = w.shape
    return w.reshape(d, n // LANES, 2, HALF)[:, :, ::-1, :].reshape(d, n)


def _split_w_in(w):
    o = 0
    parts = []
    for size in (1024, 1024, 1024, MLA_Q_RANK, MLA_KV_RANK, MLA_ROPE_DIM, 1024, 256, 256):
        parts.append(w[:, o:o + size])
        o += size
    a_q, a_k, a_v, b_cq, b_ckv, b_kr, c_q, c_k, c_v = parts
    gates = w[:, o:]
    w_rope = jnp.concatenate([a_q, a_k, c_q, c_k, _swap_halves(c_k)], axis=1).astype(BF16)
    w_plain = jnp.concatenate([a_v, c_v, _swap_halves(c_v)], axis=1).astype(BF16)
    w_lat = jnp.concatenate([b_cq, b_ckv, b_kr, jnp.zeros_like(b_kr)], axis=1).astype(BF16)
    return w_rope, w_plain, w_lat, gates.astype(BF16)


def kernel(x, c, positions, w_in, diff_lambda_q, diff_lambda_k, diff_subln, mla_q_norm, mla_kv_norm,
           mla_w_uq, mla_w_ukv, swa_sink, w_branch_a, w_branch_b, w_branch_c, w_out, w_ada, b_ada,
           ln_mix_g, ln_mix_b, ln_ffn_g, ln_ffn_b, router_w, router_bias, expert_w1, expert_w3,
           expert_w2):
    B, S, D = x.shape
    T = B * S
    depth = w_in.shape[0]
    alpha = (2 * depth) ** 0.25
    tm = min(S, 512)
    tq = min(S, 256)
    tk = min(S, 512)

    mod = _ada_mod(c, w_ada, b_ada)
    cos, sin = _rope_tables(positions)
    rw = jnp.pad(router_w, ((0, 0), (0, LANES - N_EXPERTS)))
    rwh = rw.astype(BF16)
    rwl = (rw - rwh.astype(F32)).astype(BF16)
    rb = jnp.pad(router_bias, (0, LANES - N_EXPERTS)).reshape(1, LANES)

    x2 = x.reshape(T, D)
    for l in range(depth):
        m = lambda k: mod[l, :, k, :].reshape(B, 1, D)
        lam_init = 0.8 - 0.6 * math.exp(-0.3 * l)
        w_rope, w_plain, w_lat, w_gates = _split_w_in(w_in[l])
        proj = functools.partial(_inproj, x2, m(0), m(1), cos=cos, sin=sin, tm=tm, seq=S)
        rope_out = proj(w_rope, mode="rope", out_dtype=BF16, tn=512)
        plain_out = proj(w_plain, mode="plain", out_dtype=BF16, tn=512)
        lat = proj(w_lat, mode="plain", out_dtype=F32, tn=w_lat.shape[1])
        gates = proj(w_gates, mode="sigmoid", out_dtype=BF16, tn=512)

        ya = _attn_a(jnp.array([lam_init], F32), rope_out, plain_out, diff_lambda_q[l], diff_lambda_k[l],
                     diff_subln[l].reshape(1, DIFF_V_DIM), B=B, S=S, tq=tq, tk=tk)

        wq = mla_w_uq[l].reshape(MLA_Q_RANK, MLA_HEADS, MLA_NOPE_DIM + MLA_ROPE_DIM)
        wq = jnp.concatenate([wq[:, :, :MLA_NOPE_DIM].reshape(MLA_Q_RANK, -1),
                              wq[:, :, MLA_NOPE_DIM:].reshape(MLA_Q_RANK, -1)], axis=1).astype(BF16)
        wkv = mla_w_ukv[l].reshape(MLA_KV_RANK, MLA_HEADS, MLA_NOPE_DIM + MLA_V_DIM)
        wkv = jnp.concatenate([wkv[:, :, :MLA_NOPE_DIM].reshape(MLA_KV_RANK, -1),
                               wkv[:, :, MLA_NOPE_DIM:].reshape(MLA_KV_RANK, -1)], axis=1).astype(BF16)
        qb, kb, vb = _mla_up(lat, mla_q_norm[l].reshape(1, -1), mla_kv_norm[l].reshape(1, -1), wq, wkv,
                             cos, sin, tm=tm)
        yb = _attn_b(qb, kb, vb, B=B, S=S, tq=tq, tk=tk)

        yc = _attn_c(swa_sink[l], rope_out, plain_out, B=B, S=S)

        x2 = _merge(ya, yb, yc, gates, x2, m(2), w_branch_a[l].astype(BF16), w_branch_b[l].astype(BF16),
                    w_branch_c[l].astype(BF16), w_out[l].astype(BF16), ln_mix_g[l].reshape(1, D),
                    ln_mix_b[l].reshape(1, D), alpha=alpha, tm=min(S, 256), seq=S)

        x2 = _moe(x2, m(3), m(4), m(5), rwh, rwl, rb, expert_w1[l].astype(BF16), expert_w3[l].astype(BF16),
                  expert_w2[l].astype(BF16), ln_ffn_g[l].reshape(1, D), ln_ffn_b[l].reshape(1, D),
                  alpha=alpha, tm=tm, seq=S)
    return x2.reshape(B, S, D)
```
